```python
import jax, jax.numpy as jnp
from jax import lax
import numpy as np

D_MODEL = 1024
BATCH = 16
SEQ = 2048
DEPTH = 2
DEC_BATCH = 8
DEC_SEQ = 32
PAST_LEN = 4096

CHUNK = 64
D_MIX = D_MODEL
D_POOL = D_MIX // 4
POOL_WINDOWS = (2, 4, 8, 16)
POOL_GROUPS = len(POOL_WINDOWS)
POOL_GC = D_POOL // POOL_GROUPS
POOL_PAD = max(POOL_WINDOWS) - 1
D_CONV = D_MIX // 4
CONV_WIDTH = 31
CONV_PAD = CONV_WIDTH - 1
D_ATT = D_MIX // 2
HEAD_DIM = 64
N_HEADS = D_ATT // HEAD_DIM
D_IN = D_POOL + 2 * D_CONV + 3 * D_ATT + N_HEADS
D_FF = 2816
QBLK = 128
EPS = 1e-6

kernel_name = "hybrid_streaming_encoder_step"


def rms_norm(x, g):
    xf = x.astype(jnp.float32)
    y = xf * lax.rsqrt(jnp.mean(xf * xf, axis=-1, keepdims=True) + EPS)
    return (y * g.astype(jnp.float32)).astype(x.dtype)


def layer_norm(x, g, b):
    xf = x.astype(jnp.float32)
    mu = jnp.mean(xf, axis=-1, keepdims=True)
    var = jnp.mean(jnp.square(xf - mu), axis=-1, keepdims=True)
    y = (xf - mu) * lax.rsqrt(var + EPS) * g.astype(jnp.float32) + b.astype(jnp.float32)
    return y.astype(x.dtype)


def swiglu_ffn(x, g, w_gu, w_down):
    h = rms_norm(x, g) @ w_gu
    a, u = jnp.split(h, 2, axis=-1)
    return (jax.nn.silu(a) * u) @ w_down


def pool_mixer(u, prev, t0, w, scale):
    B, T, C = u.shape
    seq = jnp.concatenate([prev.astype(u.dtype), u], axis=1)
    c = jnp.cumsum(seq.astype(jnp.float32), axis=1)
    c = jnp.concatenate([jnp.zeros((B, 1, C), jnp.float32), c], axis=1)
    P = POOL_PAD
    pos = t0 + jnp.arange(T)
    outs = []
    for g, wsz in enumerate(POOL_WINDOWS):
        sl = slice(g * POOL_GC, (g + 1) * POOL_GC)
        s = c[:, P + 1:P + 1 + T, sl] - c[:, P + 1 - wsz:P + 1 - wsz + T, sl]
        cnt = jnp.minimum(pos + 1, wsz).astype(jnp.float32)
        outs.append(s / cnt[None, :, None])
    pooled = jnp.concatenate(outs, axis=-1)
    d = (pooled - u.astype(jnp.float32)).astype(u.dtype).reshape(B, T, POOL_GROUPS, POOL_GC)
    y = jnp.einsum('btgc,gcd->btgd', d, w).reshape(B, T, C) * scale
    return y, seq[:, -P:]


def conv_mixer(u_glu, prev, w, b, ln_g, ln_b):
    a, gate = jnp.split(u_glu, 2, axis=-1)
    z = a * jax.nn.sigmoid(gate)
    seq = jnp.concatenate([prev.astype(z.dtype), z], axis=1)
    y = lax.conv_general_dilated(
        seq, w[:, None, :].astype(z.dtype), window_strides=(1,), padding='VALID',
        dimension_numbers=('NWC', 'WIO', 'NWC'), feature_group_count=D_CONV) + b
    y = jax.nn.silu(layer_norm(y, ln_g, ln_b))
    return y, seq[:, -CONV_PAD:]


def fox_attention(q, k, v, cq, ck, q_start):
    B, T, H, Dh = q.shape
    S = k.shape[1]
    blk = QBLK if T % QBLK == 0 else T
    nb = T // blk
    qb = q.reshape(B, nb, blk, H, Dh).swapaxes(0, 1)
    cqb = cq.reshape(B, nb, blk, H).swapaxes(0, 1)
    starts = q_start + jnp.arange(nb) * blk
    kf = k.astype(jnp.float32)
    ckT = ck.transpose(0, 2, 1)[:, :, None, :]
    kpos = jnp.arange(S)
    scale = HEAD_DIM ** -0.5

    def block(args):
        qi, ci, st = args
        s = jnp.einsum('bqhd,bkhd->bhqk', qi.astype(jnp.float32), kf) * scale
        s = s + ci.transpose(0, 2, 1)[..., None] - ckT
        qpos = st + jnp.arange(blk)
        mask = kpos[None, :] <= qpos[:, None]
        s = jnp.where(mask[None, None], s, -jnp.inf)
        p = jax.nn.softmax(s, axis=-1)
        return jnp.einsum('bhqk,bkhd->bqhd', p.astype(v.dtype), v)

    o = lax.map(block, (qb, cqb, starts))
    return o.swapaxes(0, 1).reshape(B, T, H, Dh)


def mixer_sublayer(x, prev_pool, prev_conv, k_past, v_past, logf_past, t0,
                   mix_norm, w_in, w_out, pool_w, pool_scale, conv_w, conv_b,
                   conv_ln_g, conv_ln_b, q_norm, k_norm, forget_b):
    B, T, _ = x.shape
    u = rms_norm(x, mix_norm) @ w_in
    cuts = np.cumsum([D_POOL, 2 * D_CONV, D_ATT, D_ATT, D_ATT]).tolist()
    u_pool, u_glu, u_q, u_k, u_v, u_f = jnp.split(u, cuts, axis=-1)
    y_pool, new_pool = pool_mixer(u_pool, prev_pool, t0, pool_w, pool_scale)
    y_conv, new_conv = conv_mixer(u_glu, prev_conv, conv_w, conv_b, conv_ln_g, conv_ln_b)
    q = rms_norm(u_q.reshape(B, T, N_HEADS, HEAD_DIM), q_norm)
    k = rms_norm(u_k.reshape(B, T, N_HEADS, HEAD_DIM), k_norm)
    v = u_v.reshape(B, T, N_HEADS, HEAD_DIM)
    logf = jax.nn.log_sigmoid((u_f + forget_b).astype(jnp.float32))
    if k_past is None:
        k_all, v_all, logf_all, q_start = k, v, logf, 0
    else:
        k_all = jnp.concatenate([k_past.astype(k.dtype), k], axis=1)
        v_all = jnp.concatenate([v_past.astype(v.dtype), v], axis=1)
        logf_all = jnp.concatenate([logf_past.astype(jnp.float32), logf], axis=1)
        q_start = k_past.shape[1]
    c_all = jnp.cumsum(logf_all, axis=1)
    y_att = fox_attention(q, k_all, v_all, c_all[:, q_start:], c_all, q_start)
    y = jnp.concatenate([y_pool, y_conv, y_att.reshape(B, T, D_ATT)], axis=-1) @ w_out
    return x + y, new_pool, new_conv, k, v, logf


def setup_inputs(seed: int = 0) -> dict:
    key = jax.random.key(seed)
    ks = jax.random.split(key, 26)
    nrm = jax.random.normal
    f32 = jnp.float32
    return {
        "x_prompt": nrm(ks[0], (BATCH, SEQ, D_MODEL), f32),
        "x_sample": nrm(ks[1], (DEC_BATCH, DEC_SEQ, D_MODEL), f32),
        "state_pool": nrm(ks[2], (DEPTH, DEC_BATCH, POOL_PAD, D_POOL), f32),
        "state_conv": 0.5 * nrm(ks[3], (DEPTH, DEC_BATCH, CONV_PAD, D_CONV), f32),
        "cache_k": nrm(ks[4], (DEPTH, DEC_BATCH, PAST_LEN, N_HEADS, HEAD_DIM), f32),
        "cache_v": nrm(ks[5], (DEPTH, DEC_BATCH, PAST_LEN, N_HEADS, HEAD_DIM), f32),
        "cache_logf": jax.nn.log_sigmoid(2.5 + nrm(ks[6], (DEPTH, DEC_BATCH, PAST_LEN, N_HEADS), f32)),
        "ffn1_norm": 1.0 + 0.1 * nrm(ks[7], (DEPTH, D_MODEL), f32),
        "ffn1_w_gu": nrm(ks[8], (DEPTH, D_MODEL, 2 * D_FF), f32) * D_MODEL ** -0.5,
        "ffn1_w_down": nrm(ks[9], (DEPTH, D_FF, D_MODEL), f32) * D_FF ** -0.5,
        "mix_norm": 1.0 + 0.1 * nrm(ks[10], (DEPTH, D_MODEL), f32),
        "w_in": nrm(ks[11], (DEPTH, D_MODEL, D_IN), f32) * D_MODEL ** -0.5,
        "w_out": nrm(ks[12], (DEPTH, D_MIX, D_MODEL), f32) * D_MIX ** -0.5,
        "pool_w": nrm(ks[13], (DEPTH, POOL_GROUPS, POOL_GC, POOL_GC), f32) * POOL_GC ** -0.5,
        "pool_scale": 1.0 + 0.1 * nrm(ks[14], (DEPTH, D_POOL), f32),
        "conv_w": nrm(ks[15], (DEPTH, CONV_WIDTH, D_CONV), f32) * CONV_WIDTH ** -0.5,
        "conv_b": 0.02 * nrm(ks[16], (DEPTH, D_CONV), f32),
        "conv_ln_g": 1.0 + 0.1 * nrm(ks[17], (DEPTH, D_CONV), f32),
        "conv_ln_b": 0.02 * nrm(ks[18], (DEPTH, D_CONV), f32),
        "q_norm": 1.0 + 0.1 * nrm(ks[19], (DEPTH, HEAD_DIM), f32),
        "k_norm": 1.0 + 0.1 * nrm(ks[20], (DEPTH, HEAD_DIM), f32),
        "forget_b": jax.random.uniform(ks[21], (DEPTH, N_HEADS), f32, minval=1.0, maxval=4.0),
        "ffn2_norm": 1.0 + 0.1 * nrm(ks[22], (DEPTH, D_MODEL), f32),
        "ffn2_w_gu": nrm(ks[23], (DEPTH, D_MODEL, 2 * D_FF), f32) * D_MODEL ** -0.5,
        "ffn2_w_down": nrm(ks[24], (DEPTH, D_FF, D_MODEL), f32) * D_FF ** -0.5,
    }


def reference(x_prompt, x_sample, state_pool, state_conv, cache_k, cache_v, cache_logf,
              ffn1_norm, ffn1_w_gu, ffn1_w_down, mix_norm, w_in, w_out, pool_w, pool_scale,
              conv_w, conv_b, conv_ln_g, conv_ln_b, q_norm, k_norm, forget_b,
              ffn2_norm, ffn2_w_gu, ffn2_w_down):
    xp, xs = x_prompt, x_sample
    Bp = xp.shape[0]
    pool_p, pool_s, conv_p, conv_s = [], [], [], []
    kp, vp, fp, ksl, vsl, fsl = [], [], [], [], [], []
    for l in range(DEPTH):
        mix_w = dict(mix_norm=mix_norm[l], w_in=w_in[l], w_out=w_out[l], pool_w=pool_w[l],
                     pool_scale=pool_scale[l], conv_w=conv_w[l], conv_b=conv_b[l],
                     conv_ln_g=conv_ln_g[l], conv_ln_b=conv_ln_b[l], q_norm=q_norm[l],
                     k_norm=k_norm[l], forget_b=forget_b[l])
        xp = xp + 0.5 * swiglu_ffn(xp, ffn1_norm[l], ffn1_w_gu[l], ffn1_w_down[l])
        xp, npool, nconv, nk, nv, nf = mixer_sublayer(
            xp, jnp.zeros((Bp, POOL_PAD, D_POOL), xp.dtype), jnp.zeros((Bp, CONV_PAD, D_CONV), xp.dtype),
            None, None, None, 0, **mix_w)
        xp = xp + 0.5 * swiglu_ffn(xp, ffn2_norm[l], ffn2_w_gu[l], ffn2_w_down[l])
        pool_p.append(npool); conv_p.append(nconv); kp.append(nk); vp.append(nv); fp.append(nf)
        xs = xs + 0.5 * swiglu_ffn(xs, ffn1_norm[l], ffn1_w_gu[l], ffn1_w_down[l])
        xs, npool, nconv, nk, nv, nf = mixer_sublayer(
            xs, state_pool[l], state_conv[l], cache_k[l], cache_v[l], cache_logf[l],
            cache_k.shape[2], **mix_w)
        xs = xs + 0.5 * swiglu_ffn(xs, ffn2_norm[l], ffn2_w_gu[l], ffn2_w_down[l])
        pool_s.append(npool); conv_s.append(nconv); ksl.append(nk); vsl.append(nv); fsl.append(nf)
    return (xp, xs,
            jnp.stack(pool_p), jnp.stack(pool_s),
            jnp.stack(conv_p), jnp.stack(conv_s),
            jnp.stack(kp), jnp.stack(vp), jnp.stack(fp),
            jnp.stack(ksl), jnp.stack(vsl), jnp.stack(fsl))
```

```python
import functools

import jax
import jax.numpy as jnp
from jax import lax
from jax.experimental import pallas as pl
from jax.experimental.pallas import tpu as pltpu

F32 = jnp.float32
BF16 = jnp.bfloat16

EPS = 1e-6
N_HEADS = 8
HEAD_DIM = 64
D_ATT = N_HEADS * HEAD_DIM
D_POOL = 256
D_CONV = 256
POOL_WINDOWS = (2, 4, 8, 16)
POOL_GC = D_POOL // len(POOL_WINDOWS)
POOL_PAD = 15
CONV_WIDTH = 31
CONV_PAD = 30
POOL_HIST = 16
CONV_HIST = 32
LANES = 128
FF_CHUNK = 256
VMEM_LIMIT_BYTES = 56 * 1024 * 1024
NEG_BIG = -1e30

C_POOL = 0
C_GLU_A = C_POOL + D_POOL
C_GLU_G = C_GLU_A + D_CONV
C_Q = C_GLU_G + D_CONV
C_K = C_Q + D_ATT
C_V = C_K + D_ATT
C_F = C_V + D_ATT
D_IN_PAD = C_F + LANES


def _rms(x, g):
    return x * lax.rsqrt(jnp.mean(x * x, axis=-1, keepdims=True) + EPS) * g


def _split3(x):
    hi = x.astype(BF16)
    r1 = x - hi.astype(F32)
    mid = r1.astype(BF16)
    lo = (r1 - mid.astype(F32)).astype(BF16)
    return hi, mid, lo


def _resident(shape):
    zeros = (0,) * len(shape)
    return pl.BlockSpec(shape, lambda *_: zeros, pipeline_mode=pl.Buffered(1))


def _params(*sem):
    return pltpu.CompilerParams(dimension_semantics=sem, vmem_limit_bytes=VMEM_LIMIT_BYTES)


def _swiglu(x, g_ref, wg_ref, wu_ref, wd_ref, acc_ref):
    xn = _rms(x, g_ref[...]).astype(BF16)
    acc_ref[...] = jnp.zeros_like(acc_ref)

    def chunk(j, carry):
        a = jnp.dot(xn, wg_ref[j], preferred_element_type=F32)
        u = jnp.dot(xn, wu_ref[j], preferred_element_type=F32)
        h = (a * jax.nn.sigmoid(a) * u).astype(BF16)
        acc_ref[...] += jnp.dot(h, wd_ref[j], preferred_element_type=F32)
        return carry

    lax.fori_loop(0, wg_ref.shape[0], chunk, 0)
    return x + 0.5 * acc_ref[...]


def _ffn_kernel(x_ref, g_ref, wg_ref, wu_ref, wd_ref, o_ref, acc_ref):
    o_ref[...] = _swiglu(x_ref[...], g_ref, wg_ref, wu_ref, wd_ref, acc_ref)


def _mix_ffn_kernel(x_ref, ypc_ref, yat_ref, wo_ref, g_ref, wg_ref, wu_ref, wd_ref, o_ref, acc_ref):
    d_pc = ypc_ref.shape[-1]
    x = x_ref[...]
    x = x + jnp.dot(ypc_ref[...], wo_ref[0:d_pc, :], preferred_element_type=F32)
    x = x + jnp.dot(yat_ref[...], wo_ref[d_pc:, :], preferred_element_type=F32)
    o_ref[...] = _swiglu(x, g_ref, wg_ref, wu_ref, wd_ref, acc_ref)


def _ffn_call(x, ffn_w, mix=None):
    g, wg, wu, wd = ffn_w
    m, d = x.shape
    tm = min(512, m)
    row = lambda c: pl.BlockSpec((tm, c), lambda i: (i, 0))
    w_specs = [_resident(g.shape), _resident(wg.shape), _resident(wu.shape), _resident(wd.shape)]
    if mix is None:
        body, args, specs = _ffn_kernel, (x,), [row(d)]
    else:
        ypc, yat, wo = mix
        body, args = _mix_ffn_kernel, (x, ypc, yat, wo)
        specs = [row(d), row(ypc.shape[1]), row(yat.shape[1]), _resident(wo.shape)]
    return pl.pallas_call(
        body,
        grid=(m // tm,),
        in_specs=specs + w_specs,
        out_specs=row(d),
        out_shape=jax.ShapeDtypeStruct((m, d), F32),
        scratch_shapes=[pltpu.VMEM((tm, d), F32)],
        compiler_params=_params("parallel"),
    )(*args, g, wg, wu, wd)


def _head_rms(u, gsum, gain):
    half = gsum.shape[0]
    outs = []
    for s in range(u.shape[1] // half):
        xh = u[:, s * half:(s + 1) * half]
        ss = jnp.dot((xh * xh).astype(BF16), gsum, preferred_element_type=F32)
        outs.append(xh * lax.rsqrt(ss * (1.0 / HEAD_DIM) + EPS))
    return jnp.concatenate(outs, axis=-1) * gain


def _mixer_in_kernel(x_ref, g_ref, win_ref, pprev_ref, cprev_ref, c0_ref, wbd_ref, pscale_ref,
                     convw_ref, convb_ref, lng_ref, lnb_ref, qg_ref, kg_ref, fb_ref, gsum_ref, tri_ref,
                     ypc_ref, q_ref, k_ref, v_ref, logf_ref, c_ref, ptail_ref, ctail_ref,
                     pseq_ref, cseq_ref, carry_ref, *, tt, t0):
    ti = pl.program_id(1)

    @pl.when(ti == 0)
    def _():
        pseq_ref[0:POOL_HIST, :] = pprev_ref[...]
        cseq_ref[0:CONV_HIST, :] = cprev_ref[...]
        carry_ref[...] = c0_ref[...]

    xn = _rms(x_ref[...], g_ref[...]).astype(BF16)

    def proj(lo, hi):
        return jnp.dot(xn, win_ref[:, lo:hi], preferred_element_type=F32)

    up = proj(C_POOL, C_GLU_A)
    pseq_ref[POOL_HIST:POOL_HIST + tt, :] = up
    run = up
    win_sum = {}
    for k in range(1, max(POOL_WINDOWS)):
        run = run + pseq_ref[pl.ds(POOL_HIST - k, tt), :]
        if k + 1 in POOL_WINDOWS:
            win_sum[k + 1] = run
    pos = t0 + ti * tt + lax.broadcasted_iota(jnp.int32, (tt, 1), 0)
    lane = lax.broadcasted_iota(jnp.int32, (1, D_POOL), 1)
    pooled = None
    for gi in reversed(range(len(POOL_WINDOWS))):
        w = POOL_WINDOWS[gi]
        mean = win_sum[w] * (1.0 / jnp.minimum(pos + 1, w).astype(F32))
        pooled = mean if pooled is None else jnp.where(lane < (gi + 1) * POOL_GC, mean, pooled)
    d = (pooled - up).astype(BF16)
    y_pool = jnp.dot(d, wbd_ref[...], preferred_element_type=F32) * pscale_ref[...]
    ypc_ref[:, 0:D_POOL] = y_pool.astype(ypc_ref.dtype)
    ptail_ref[...] = up[tt - POOL_HIST:tt, :]
    pseq_ref[0:POOL_HIST, :] = pseq_ref[tt:tt + POOL_HIST, :]

    z = proj(C_GLU_A, C_GLU_G) * jax.nn.sigmoid(proj(C_GLU_G, C_Q))
    cseq_ref[CONV_HIST:CONV_HIST + tt, :] = z
    yc = jnp.broadcast_to(convb_ref[...], (tt, D_CONV))
    for j in range(CONV_WIDTH):
        yc = yc + convw_ref[j:j + 1, :] * cseq_ref[pl.ds(CONV_HIST - CONV_PAD + j, tt), :]
    mu = jnp.mean(yc, axis=-1, keepdims=True)
    yc = yc - mu
    var = jnp.mean(yc * yc, axis=-1, keepdims=True)
    yc = yc * lax.rsqrt(var + EPS) * lng_ref[...] + lnb_ref[...]
    ypc_ref[:, D_POOL:D_POOL + D_CONV] = (yc * jax.nn.sigmoid(yc)).astype(ypc_ref.dtype)
    ctail_ref[...] = z[tt - CONV_HIST:tt, :]
    cseq_ref[0:CONV_HIST, :] = cseq_ref[tt:tt + CONV_HIST, :]

    gsum = gsum_ref[...]
    q_ref[...] = (_head_rms(proj(C_Q, C_K), gsum, qg_ref[...]) * (HEAD_DIM ** -0.5)).astype(q_ref.dtype)
    k_ref[...] = _head_rms(proj(C_K, C_V), gsum, kg_ref[...])
    v_ref[...] = proj(C_V, C_F)
    xf = proj(C_F, D_IN_PAD) + fb_ref[...]
    lf = jnp.minimum(xf, 0.0) - jnp.log1p(jnp.exp(-jnp.abs(xf)))
    logf_ref[...] = lf[:, 0:N_HEADS]
    tri = tri_ref[...]
    c = carry_ref[...]
    for part in _split3(lf):
        c = c + jnp.dot(tri, part, preferred_element_type=F32)
    c_ref[...] = c[:, 0:N_HEADS]
    carry_ref[...] = c[tt - 1:tt, :]


def _mixer_in_call(x, mw, pool_prev, conv_prev, c0, t0):
    b, t, d = x.shape
    tt = min(512, t)
    tri = jnp.tril(jnp.ones((tt, tt), BF16))
    weights = (mw["norm"], mw["w_in"])
    small = (mw["wbd"], mw["pscale"], mw["conv_w"], mw["conv_b"], mw["ln_g"], mw["ln_b"],
             mw["q_gain"], mw["k_gain"], mw["fb"], mw["gsum"], tri)
    per_b = lambda r, c: pl.BlockSpec((None, r, c), lambda bi, ti: (bi, 0, 0))
    tile = lambda c: pl.BlockSpec((None, tt, c), lambda bi, ti: (bi, ti, 0))
    out_shape = (
        jax.ShapeDtypeStruct((b, t, D_POOL + D_CONV), BF16),
        jax.ShapeDtypeStruct((b, t, D_ATT), BF16),
        jax.ShapeDtypeStruct((b, t, D_ATT), F32),
        jax.ShapeDtypeStruct((b, t, D_ATT), F32),
        jax.ShapeDtypeStruct((b, t, N_HEADS), F32),
        jax.ShapeDtypeStruct((b, t, N_HEADS), F32),
        jax.ShapeDtypeStruct((b, POOL_HIST, D_POOL), F32),
        jax.ShapeDtypeStruct((b, CONV_HIST, D_CONV), F32),
    )
    out_specs = (tile(D_POOL + D_CONV), tile(D_ATT), tile(D_ATT), tile(D_ATT), tile(N_HEADS), tile(N_HEADS),
                 per_b(POOL_HIST, D_POOL), per_b(CONV_HIST, D_CONV))
    return pl.pallas_call(
        functools.partial(_mixer_in_kernel, tt=tt, t0=t0),
        grid=(b, t // tt),
        in_specs=[tile(d)] + [_resident(w.shape) for w in weights]
        + [per_b(POOL_HIST, D_POOL), per_b(CONV_HIST, D_CONV), per_b(1, LANES)]
        + [_resident(w.shape) for w in small],
        out_specs=out_specs,
        out_shape=out_shape,
        scratch_shapes=[pltpu.VMEM((POOL_HIST + tt, D_POOL), F32),
                        pltpu.VMEM((CONV_HIST + tt, D_CONV), F32),
                        pltpu.VMEM((1, LANES), F32)],
        compiler_params=_params("parallel", "arbitrary"),
    )(x, *weights, pool_prev, conv_prev, c0, *small)


def _aug_lanes(e):
    base = HEAD_DIM * (1 - e)
    return base, lax.broadcasted_iota(jnp.int32, (1, LANES), 1)


def _aug_k(k_pair, c_col, e):
    base, lane = _aug_lanes(e)
    in_head = (lane >= HEAD_DIM * e) & (lane < HEAD_DIM * (e + 1))
    out = jnp.where(in_head, k_pair, 0.0)
    for i, part in enumerate(_split3(c_col)):
        out = jnp.where(lane == base + i, -part.astype(F32), out)
    out = jnp.where((lane >= base + 3) & (lane < base + 6), 1.0, out)
    return out.astype(BF16)


def _aug_q(q_pair, c_col, e):
    base, lane = _aug_lanes(e)
    in_head = (lane >= HEAD_DIM * e) & (lane < HEAD_DIM * (e + 1))
    out = jnp.where(in_head, q_pair.astype(F32), 0.0)
    out = jnp.where((lane >= base) & (lane < base + 3), 1.0, out)
    for i, part in enumerate(_split3(c_col)):
        out = jnp.where(lane == base + 3 + i, part.astype(F32), out)
    return out.astype(BF16)


def _attend(qa, ka, vb, state, mask=None):
    m, l, acc = state
    s = lax.dot_general(qa, ka, (((1,), (1,)), ((), ())), preferred_element_type=F32)
    if mask is not None:
        s = jnp.where(mask, s, NEG_BIG)
    m_new = jnp.maximum(m, jnp.max(s, axis=-1, keepdims=True))
    alpha = jnp.exp(m - m_new)
    p = jnp.exp(s - m_new)
    l = alpha * l + jnp.sum(p, axis=-1, keepdims=True)
    acc = alpha * acc + jnp.dot(p.astype(BF16), vb, preferred_element_type=F32)
    return m_new, l, acc


def _init_state(tq):
    return (jnp.full((tq, 1), NEG_BIG, F32), jnp.zeros((tq, 1), F32), jnp.zeros((tq, LANES), F32))


def _causal_mask(tq, tk):
    return lax.broadcasted_iota(jnp.int32, (tq, tk), 1) <= lax.broadcasted_iota(jnp.int32, (tq, tk), 0)


def _merge_pair(states):
    lane = lax.broadcasted_iota(jnp.int32, (1, LANES), 1)
    (_, l0, a0), (_, l1, a1) = states
    return jnp.where(lane < HEAD_DIM, a0 * (1.0 / l0), a1 * (1.0 / l1))


def _attn_prompt_kernel(q_ref, k_ref, v_ref, c_ref, o_ref, ka_ref, vb_ref, *, tq):
    qi = pl.program_id(1)
    t = k_ref.shape[0]

    @pl.when(qi == 0)
    def _():
        vb_ref[...] = v_ref[...].astype(BF16)
        for h in range(N_HEADS):
            p, e = divmod(h, 2)
            ka_ref[h] = _aug_k(k_ref[:, p * LANES:(p + 1) * LANES], c_ref[:, h:h + 1], e)

    row0 = pl.multiple_of(qi * tq, tq)
    mask = _causal_mask(tq, tq)
    for p in range(N_HEADS // 2):
        lanes = slice(p * LANES, (p + 1) * LANES)
        states = []
        for e in range(2):
            h = 2 * p + e
            qa = _aug_q(q_ref[:, lanes], c_ref[pl.ds(row0, tq), h:h + 1], e)

            def step(j, state, h=h, qa=qa, lanes=lanes):
                r = pl.multiple_of(j * tq, tq)
                return _attend(qa, ka_ref[h, pl.ds(r, tq), :], vb_ref[pl.ds(r, tq), lanes], state)

            state = lax.fori_loop(0, qi, step, _init_state(tq))
            states.append(_attend(qa, ka_ref[h, pl.ds(row0, tq), :], vb_ref[pl.ds(row0, tq), lanes], state, mask))
        o_ref[:, lanes] = _merge_pair(states).astype(o_ref.dtype)


def _attn_prompt_call(q, k, v, c):
    b, t, _ = q.shape
    tq = min(256, t)
    per_b = lambda cols: pl.BlockSpec((None, t, cols), lambda bi, qi: (bi, 0, 0))
    tile = pl.BlockSpec((None, tq, D_ATT), lambda bi, qi: (bi, qi, 0))
    return pl.pallas_call(
        functools.partial(_attn_prompt_kernel, tq=tq),
        grid=(b, t // tq),
        in_specs=[tile, per_b(D_ATT), per_b(D_ATT), per_b(N_HEADS)],
        out_specs=tile,
        out_shape=jax.ShapeDtypeStruct((b, t, D_ATT), BF16),
        scratch_shapes=[pltpu.VMEM((N_HEADS, t, LANES), BF16), pltpu.VMEM((t, D_ATT), BF16)],
        compiler_params=_params("parallel", "arbitrary"),
    )(q, k, v, c)


def _attn_sample_kernel(q_ref, kc_ref, vc_ref, cc_ref, kn_ref, vn_ref, cn_ref, o_ref, *, tk):
    tq = q_ref.shape[0]
    n_past = kc_ref.shape[0]
    mask = _causal_mask(tq, tq)
    for p in range(N_HEADS // 2):
        lanes = slice(p * LANES, (p + 1) * LANES)
        states = []
        for e in range(2):
            h = 2 * p + e
            qa = _aug_q(q_ref[:, lanes], cn_ref[:, h:h + 1], e)

            def step(j, state, h=h, qa=qa, lanes=lanes, e=e):
                r = pl.multiple_of(j * tk, tk)
                ka = _aug_k(kc_ref[pl.ds(r, tk), lanes], cc_ref[pl.ds(r, tk), h:h + 1], e)
                return _attend(qa, ka, vc_ref[pl.ds(r, tk), lanes].astype(BF16), state)

            state = lax.fori_loop(0, n_past // tk, step, _init_state(tq))
            ka = _aug_k(kn_ref[:, lanes], cn_ref[:, h:h + 1], e)
            states.append(_attend(qa, ka, vn_ref[:, lanes].astype(BF16), state, mask))
        o_ref[:, lanes] = _merge_pair(states).astype(o_ref.dtype)


def _attn_sample_call(q, k_cache, v_cache, c_cache, k_new, v_new, c_new):
    b, tq, _ = q.shape
    n_past = k_cache.shape[1]
    tk = min(1024, n_past)
    blk = lambda r, cols: pl.BlockSpec((None, r, cols), lambda bi: (bi, 0, 0))
    return pl.pallas_call(
        functools.partial(_attn_sample_kernel, tk=tk),
        grid=(b,),
        in_specs=[blk(tq, D_ATT), blk(n_past, D_ATT), blk(n_past, D_ATT), blk(n_past, N_HEADS),
                  blk(tq, D_ATT), blk(tq, D_ATT), blk(tq, N_HEADS)],
        out_specs=blk(tq, D_ATT),
        out_shape=jax.ShapeDtypeStruct((b, tq, D_ATT), BF16),
        compiler_params=_params("parallel"),
    )(q, k_cache, v_cache, c_cache, k_new, v_new, c_new)


def _cumsum_kernel(x_ref, tri_ref, c_ref, carry_ref):
    @pl.when(pl.program_id(1) == 0)
    def _():
        carry_ref[...] = jnp.zeros_like(carry_ref)

    tt = x_ref.shape[0]
    c = carry_ref[...]
    for part in _split3(x_ref[...]):
        c = c + jnp.dot(tri_ref[...], part, preferred_element_type=F32)
    c_ref[...] = c[:, 0:N_HEADS]
    carry_ref[...] = c[tt - 1:tt, :]


def _cumsum_call(x):
    b, s, _ = x.shape
    tt = min(512, s)
    tri = jnp.tril(jnp.ones((tt, tt), BF16))
    return pl.pallas_call(
        _cumsum_kernel,
        grid=(b, s // tt),
        in_specs=[pl.BlockSpec((None, tt, LANES), lambda bi, ti: (bi, ti, 0)), _resident(tri.shape)],
        out_specs=pl.BlockSpec((None, tt, N_HEADS), lambda bi, ti: (bi, ti, 0)),
        out_shape=jax.ShapeDtypeStruct((b, s, N_HEADS), F32),
        scratch_shapes=[pltpu.VMEM((1, LANES), F32)],
        compiler_params=_params("parallel", "arbitrary"),
    )(x, tri)


def _pack_ffn(norm, w_gu, w_down):
    d, f2 = w_gu.shape
    f = f2 // 2
    n = f // FF_CHUNK
    chunks = lambda w: w.reshape(d, n, FF_CHUNK).transpose(1, 0, 2).astype(BF16)
    return (norm.reshape(1, d), chunks(w_gu[:, :f]), chunks(w_gu[:, f:]),
            w_down.reshape(n, FF_CHUNK, d).astype(BF16))


def _pack_mixer(mix_norm, w_in, w_out, pool_w, pool_scale, conv_w, conv_b, ln_g, ln_b, q_norm, k_norm, forget_b):
    d = w_in.shape[0]
    w_in_p = jnp.pad(w_in, ((0, 0), (0, D_IN_PAD - w_in.shape[1]))).astype(BF16)
    wbd = jax.scipy.linalg.block_diag(*[pool_w[g] for g in range(pool_w.shape[0])]).astype(BF16)
    group = lax.broadcasted_iota(jnp.int32, (256, 256), 0) // HEAD_DIM
    gsum = (group == group.T).astype(BF16)
    row = lambda a: a.reshape(1, -1)
    return dict(
        norm=row(mix_norm), w_in=w_in_p, w_out=w_out.astype(BF16), wbd=wbd, pscale=row(pool_scale),
        conv_w=conv_w, conv_b=row(conv_b), ln_g=row(ln_g), ln_b=row(ln_b),
        q_gain=row(jnp.tile(q_norm, N_HEADS)), k_gain=row(jnp.tile(k_norm, N_HEADS)),
        fb=row(jnp.pad(forget_b, (0, LANES - N_HEADS))), gsum=gsum)


def _pad_rows(a, rows):
    return jnp.pad(a, ((0, 0), (rows - a.shape[1], 0), (0, 0)))


def _layer(x, ffn1, mw, ffn2, pool_prev, conv_prev, cache, t0):
    b, t, d = x.shape
    x1 = _ffn_call(x.reshape(b * t, d), ffn1)
    if cache is None:
        c0 = jnp.zeros((b, 1, LANES), F32)
    else:
        k_past, v_past, logf_past = cache
        n_past = k_past.shape[1]
        c_past = _cumsum_call(jnp.pad(logf_past, ((0, 0), (0, 0), (0, LANES - N_HEADS))))
        c0 = jnp.pad(c_past[:, n_past - 1:, :], ((0, 0), (0, 0), (0, LANES - N_HEADS)))
    ypc, q, k, v, logf, c, ptail, ctail = _mixer_in_call(
        x1.reshape(b, t, d), mw, _pad_rows(pool_prev, POOL_HIST), _pad_rows(conv_prev, CONV_HIST), c0, t0)
    if cache is None:
        yat = _attn_prompt_call(q, k, v, c)
    else:
        yat = _attn_sample_call(q, k_past.reshape(b, n_past, D_ATT), v_past.reshape(b, n_past, D_ATT),
                                c_past, k, v, c)
    y = _ffn_call(x1, ffn2, mix=(ypc.reshape(b * t, -1), yat.reshape(b * t, -1), mw["w_out"]))
    heads = lambda a: a.reshape(b, t, N_HEADS, HEAD_DIM)
    return (y.reshape(b, t, d), ptail[:, POOL_HIST - POOL_PAD:], ctail[:, CONV_HIST - CONV_PAD:],
            heads(k), heads(v), logf)


def kernel(x_prompt, x_sample, state_pool, state_conv, cache_k, cache_v, cache_logf, ffn1_norm, ffn1_w_gu, ffn1_w_down, mix_norm, w_in, w_out, pool_w, pool_scale, conv_w, conv_b, conv_ln_g, conv_ln_b, q_norm, k_norm, forget_b, ffn2_norm, ffn2_w_gu, ffn2_w_down):
    xp, xs = x_prompt, x_sample
    bp = xp.shape[0]
    depth = w_in.shape[0]
    n_past = cache_k.shape[2]
    outs_p, outs_s = [], []
    for l in range(depth):
        ffn1 = _pack_ffn(ffn1_norm[l], ffn1_w_gu[l], ffn1_w_down[l])
        ffn2 = _pack_ffn(ffn2_norm[l], ffn2_w_gu[l], ffn2_w_down[l])
        mw = _pack_mixer(mix_norm[l], w_in[l], w_out[l], pool_w[l], pool_scale[l], conv_w[l], conv_b[l],
                         conv_ln_g[l], conv_ln_b[l], q_norm[l], k_norm[l], forget_b[l])
        xp, *rest = _layer(xp, ffn1, mw, ffn2, jnp.zeros((bp, POOL_PAD, D_POOL), F32),
                           jnp.zeros((bp, CONV_PAD, D_CONV), F32), None, 0)
        outs_p.append(rest)
        xs, *rest = _layer(xs, ffn1, mw, ffn2, state_pool[l], state_conv[l],
                           (cache_k[l], cache_v[l], cache_logf[l]), n_past)
        outs_s.append(rest)
    stack = lambda outs, i: jnp.stack([o[i] for o in outs])
    return (xp, xs,
            stack(outs_p, 0), stack(outs_s, 0),
            stack(outs_p, 1), stack(outs_s, 1),
            stack(outs_p, 2), stack(outs_p, 3), stack(outs_p, 4),
            stack(outs_s, 2), stack(outs_s, 3), stack(outs_s, 4))
```

```python
import functools

import jax
import jax.numpy as jnp
from jax import lax
from jax.experimental import pallas as pl
from jax.experimental.pallas import tpu as pltpu

F32 = jnp.float32
BF16 = jnp.bfloat16

EPS = 1e-6
N_HEADS = 8
HEAD_DIM = 64
V_ROWS = HEAD_DIM + 16
D_ATT = N_HEADS * HEAD_DIM
D_POOL = 256
D_CONV = 256
POOL_WINDOWS = (2, 4, 8, 16)
POOL_GC = D_POOL // len(POOL_WINDOWS)
POOL_PAD = 15
CONV_WIDTH = 31
CONV_PAD = 30
POOL_HIST = 16
CONV_HIST = 32
LANES = 128
SUBLANES = 8
FF_CHUNK = 256
VMEM_LIMIT_BYTES = 56 * 1024 * 1024
NEG_BIG = -1e30
LOG2E = 1.4426950408889634

C_POOL = 0
C_GLU_A = C_POOL + D_POOL
C_GLU_G = C_GLU_A + D_CONV
C_Q = C_GLU_G + D_CONV
C_K = C_Q + D_ATT
C_V = C_K + D_ATT
C_F = C_V + D_ATT
D_IN_PAD = C_F + LANES


def _rms(x, g):
    return x * lax.rsqrt(jnp.mean(x * x, axis=-1, keepdims=True) + EPS) * g


def _split3(x):
    hi = x.astype(BF16)
    r1 = x - hi.astype(F32)
    mid = r1.astype(BF16)
    lo = (r1 - mid.astype(F32)).astype(BF16)
    return hi, mid, lo


def _resident(shape):
    zeros = (0,) * len(shape)
    return pl.BlockSpec(shape, lambda *_: zeros, pipeline_mode=pl.Buffered(1))


def _params(*sem):
    return pltpu.CompilerParams(dimension_semantics=sem, vmem_limit_bytes=VMEM_LIMIT_BYTES)


def _swiglu(x, g_ref, wgu_ref, wd_ref):
    xn = _rms(x, g_ref[...]).astype(BF16)
    f = wd_ref.shape[0]
    acc = None
    for lo in range(0, f, FF_CHUNK):
        a = jnp.dot(xn, wgu_ref[:, lo:lo + FF_CHUNK], preferred_element_type=F32)
        u = jnp.dot(xn, wgu_ref[:, f + lo:f + lo + FF_CHUNK], preferred_element_type=F32)
        h = (a * jax.nn.sigmoid(a) * u).astype(BF16)
        d = jnp.dot(h, wd_ref[lo:lo + FF_CHUNK, :], preferred_element_type=F32)
        acc = d if acc is None else acc + d
    return x + 0.5 * acc


def _ffn_kernel(x_ref, g_ref, wgu_ref, wd_ref, o_ref):
    o_ref[...] = _swiglu(x_ref[...], g_ref, wgu_ref, wd_ref)


def _mix_ffn_kernel(x_ref, ypc_ref, yat_ref, wo_ref, g_ref, wgu_ref, wd_ref, o_ref):
    d_pc = ypc_ref.shape[-1]
    x = x_ref[...]
    x = x + jnp.dot(ypc_ref[...], wo_ref[0:d_pc, :], preferred_element_type=F32)
    x = x + jnp.dot(yat_ref[...], wo_ref[d_pc:, :], preferred_element_type=F32)
    o_ref[...] = _swiglu(x, g_ref, wgu_ref, wd_ref)


def _ffn_call(x, ffn_w, mix=None):
    m, d = x.shape
    tm = min(512, m)
    row = lambda c: pl.BlockSpec((tm, c), lambda i: (i, 0))
    w_specs = [_resident(w.shape) for w in ffn_w]
    if mix is None:
        body, args, specs = _ffn_kernel, (x,), [row(d)]
    else:
        ypc, yat, wo = mix
        body, args = _mix_ffn_kernel, (x, ypc, yat, wo)
        specs = [row(d), row(ypc.shape[1]), row(yat.shape[1]), _resident(wo.shape)]
    return pl.pallas_call(
        body,
        grid=(m // tm,),
        in_specs=specs + w_specs,
        out_specs=row(d),
        out_shape=jax.ShapeDtypeStruct((m, d), F32),
        compiler_params=_params("parallel"),
    )(*args, *ffn_w)


def _head_rms(u, gsum, gain):
    half = gsum.shape[0]
    outs = []
    for s in range(u.shape[1] // half):
        xh = u[:, s * half:(s + 1) * half]
        ss = jnp.dot((xh * xh).astype(BF16), gsum, preferred_element_type=F32)
        outs.append(xh * lax.rsqrt(ss * (1.0 / HEAD_DIM) + EPS))
    return jnp.concatenate(outs, axis=-1) * gain


def _mixer_in_kernel(x_ref, g_ref, win_ref, pprev_ref, cprev_ref, c0_ref, wbd_ref, pscale_ref,
                     convw_ref, convb_ref, lng_ref, lnb_ref, qg_ref, kg_ref, fb_ref, gsum_ref, tri_ref,
                     ypc_ref, q_ref, k_ref, v_ref, logf_ref, c_ref, ptail_ref, ctail_ref,
                     psum_ref, cseq_ref, cshift_ref, carry_ref, *, tt, t0):
    ti = pl.program_id(1)
    cur = SUBLANES + POOL_HIST

    @pl.when(ti == 0)
    def _():
        for i in range(len(POOL_WINDOWS)):
            psum_ref[i, 0:SUBLANES, :] = jnp.zeros((SUBLANES, D_POOL), F32)
        psum_ref[0, SUBLANES:cur, :] = pprev_ref[...]
        cseq_ref[0:CONV_HIST, :] = cprev_ref[...]
        carry_ref[...] = c0_ref[...]

    xn = _rms(x_ref[...], g_ref[...]).astype(BF16)

    def proj(lo, hi):
        return jnp.dot(xn, win_ref[:, lo:hi], preferred_element_type=F32)

    up = proj(C_POOL, C_GLU_A)
    psum_ref[0, cur:cur + tt, :] = up
    n = POOL_HIST + tt
    win_sum = {}
    for i, w in enumerate(POOL_WINDOWS):
        half = w // 2
        s = psum_ref[i, SUBLANES:SUBLANES + n, :] + psum_ref[i, pl.ds(SUBLANES - half, n), :]
        win_sum[w] = s[POOL_HIST:, :]
        if i + 1 < len(POOL_WINDOWS):
            psum_ref[i + 1, SUBLANES:SUBLANES + n, :] = s
    pos = t0 + ti * tt + lax.broadcasted_iota(jnp.int32, (tt, 1), 0)
    lane = lax.broadcasted_iota(jnp.int32, (1, D_POOL), 1)
    pooled = None
    for gi in reversed(range(len(POOL_WINDOWS))):
        w = POOL_WINDOWS[gi]
        mean = win_sum[w] * (1.0 / jnp.minimum(pos + 1, w).astype(F32))
        pooled = mean if pooled is None else jnp.where(lane < (gi + 1) * POOL_GC, mean, pooled)
    d = (pooled - up).astype(BF16)
    y_pool = jnp.dot(d, wbd_ref[...], preferred_element_type=F32) * pscale_ref[...]
    ypc_ref[:, 0:D_POOL] = y_pool.astype(ypc_ref.dtype)
    ptail_ref[...] = up[tt - POOL_HIST:tt, :]
    psum_ref[0, SUBLANES:cur, :] = psum_ref[0, tt + SUBLANES:tt + cur, :]

    z = proj(C_GLU_A, C_GLU_G) * jax.nn.sigmoid(proj(C_GLU_G, C_Q))
    cseq_ref[CONV_HIST:CONV_HIST + tt, :] = z
    n_sh = tt + CONV_HIST - SUBLANES
    for b in range(1, SUBLANES):
        cshift_ref[b - 1] = cseq_ref[pl.ds(b, n_sh), :]
    yc = jnp.broadcast_to(convb_ref[...], (tt, D_CONV))
    for j in range(CONV_WIDTH):
        a, b = divmod(CONV_HIST - CONV_PAD + j, SUBLANES)
        rows = slice(a * SUBLANES, a * SUBLANES + tt)
        tap = cseq_ref[rows, :] if b == 0 else cshift_ref[b - 1, rows, :]
        yc = yc + convw_ref[j:j + 1, :] * tap
    mu = jnp.mean(yc, axis=-1, keepdims=True)
    yc = yc - mu
    var = jnp.mean(yc * yc, axis=-1, keepdims=True)
    yc = yc * lax.rsqrt(var + EPS) * lng_ref[...] + lnb_ref[...]
    ypc_ref[:, D_POOL:D_POOL + D_CONV] = (yc * jax.nn.sigmoid(yc)).astype(ypc_ref.dtype)
    ctail_ref[...] = z[tt - CONV_HIST:tt, :]
    cseq_ref[0:CONV_HIST, :] = cseq_ref[tt:tt + CONV_HIST, :]

    gsum = gsum_ref[...]
    q_ref[...] = (_head_rms(proj(C_Q, C_K), gsum, qg_ref[...]) * (HEAD_DIM ** -0.5 * LOG2E)).astype(q_ref.dtype)
    k_ref[...] = _head_rms(proj(C_K, C_V), gsum, kg_ref[...])
    v_ref[...] = proj(C_V, C_F)
    xf = proj(C_F, D_IN_PAD) + fb_ref[...]
    lf = jnp.minimum(xf, 0.0) - jnp.log1p(jnp.exp(-jnp.abs(xf)))
    logf_ref[...] = lf[:, 0:N_HEADS]
    tri = tri_ref[...]
    c = carry_ref[...]
    for part in _split3(lf):
        c = c + jnp.dot(tri, part, preferred_element_type=F32)
    c_ref[...] = c[:, 0:N_HEADS]
    carry_ref[...] = c[tt - 1:tt, :]


def _mixer_in_call(x, mw, pool_prev, conv_prev, c0, t0):
    b, t, d = x.shape
    tt = min(512, t)
    tri = jnp.tril(jnp.ones((tt, tt), BF16))
    weights = (mw["norm"], mw["w_in"])
    small = (mw["wbd"], mw["pscale"], mw["conv_w"], mw["conv_b"], mw["ln_g"], mw["ln_b"],
             mw["q_gain"], mw["k_gain"], mw["fb"], mw["gsum"], tri)
    per_b = lambda r, c: pl.BlockSpec((None, r, c), lambda bi, ti: (bi, 0, 0))
    tile = lambda c: pl.BlockSpec((None, tt, c), lambda bi, ti: (bi, ti, 0))
    out_shape = (
        jax.ShapeDtypeStruct((b, t, D_POOL + D_CONV), BF16),
        jax.ShapeDtypeStruct((b, t, D_ATT), BF16),
        jax.ShapeDtypeStruct((b, t, D_ATT), F32),
        jax.ShapeDtypeStruct((b, t, D_ATT), F32),
        jax.ShapeDtypeStruct((b, t, N_HEADS), F32),
        jax.ShapeDtypeStruct((b, t, N_HEADS), F32),
        jax.ShapeDtypeStruct((b, POOL_HIST, D_POOL), F32),
        jax.ShapeDtypeStruct((b, CONV_HIST, D_CONV), F32),
    )
    out_specs = (tile(D_POOL + D_CONV), tile(D_ATT), tile(D_ATT), tile(D_ATT), tile(N_HEADS), tile(N_HEADS),
                 per_b(POOL_HIST, D_POOL), per_b(CONV_HIST, D_CONV))
    return pl.pallas_call(
        functools.partial(_mixer_in_kernel, tt=tt, t0=t0),
        grid=(b, t // tt),
        in_specs=[tile(d)] + [_resident(w.shape) for w in weights]
        + [per_b(POOL_HIST, D_POOL), per_b(CONV_HIST, D_CONV), per_b(1, LANES)]
        + [_resident(w.shape) for w in small],
        out_specs=out_specs,
        out_shape=out_shape,
        scratch_shapes=[pltpu.VMEM((len(POOL_WINDOWS), SUBLANES + POOL_HIST + tt, D_POOL), F32),
                        pltpu.VMEM((CONV_HIST + tt, D_CONV), F32),
                        pltpu.VMEM((SUBLANES - 1, CONV_HIST - SUBLANES + tt, D_CONV), F32),
                        pltpu.VMEM((1, LANES), F32)],
        compiler_params=_params("parallel", "arbitrary"),
    )(x, *weights, pool_prev, conv_prev, c0, *small)


def _aug_lanes(e):
    base = HEAD_DIM * (1 - e)
    return base, lax.broadcasted_iota(jnp.int32, (1, LANES), 1)


def _c_parts(c):
    return [part.astype(F32) for part in _split3(c * LOG2E)]


def _aug_k(k_pair, c_parts, h):
    e = h % 2
    base, lane = _aug_lanes(e)
    in_head = (lane >= HEAD_DIM * e) & (lane < HEAD_DIM * (e + 1))
    out = jnp.where(in_head, k_pair, 0.0)
    for i, part in enumerate(c_parts):
        out = jnp.where(lane == base + i, -part[:, h:h + 1], out)
    out = jnp.where((lane >= base + 3) & (lane < base + 6), 1.0, out)
    return out.astype(BF16)


def _aug_q(q_pair, c_parts, h):
    e = h % 2
    base, lane = _aug_lanes(e)
    in_head = (lane >= HEAD_DIM * e) & (lane < HEAD_DIM * (e + 1))
    out = jnp.where(in_head, q_pair.astype(F32), 0.0)
    out = jnp.where((lane >= base) & (lane < base + 3), 1.0, out)
    for i, part in enumerate(c_parts):
        out = jnp.where(lane == base + 3 + i, part[:, h:h + 1], out)
    return out.astype(BF16)


def _attend(qa, ka, vb, state, mask=None):
    m, l, acc = state
    s = lax.dot_general(qa, ka, (((1,), (1,)), ((), ())), preferred_element_type=F32)
    if mask is not None:
        s = jnp.where(mask, s, NEG_BIG)
    m_new = jnp.maximum(m, jnp.max(s, axis=-1, keepdims=True))
    alpha = jnp.exp2(m - m_new)
    p = jnp.exp2(s - m_new)
    l = alpha * l + jnp.sum(p, axis=-1, keepdims=True)
    acc = alpha * acc + jnp.dot(p.astype(BF16), vb, preferred_element_type=F32)
    return m_new, l, acc


def _init_state(tq):
    return (jnp.full((tq, 1), NEG_BIG, F32), jnp.zeros((tq, 1), F32), jnp.zeros((tq, LANES), F32))


def _causal_mask(tq, tk):
    return lax.broadcasted_iota(jnp.int32, (tq, tk), 1) <= lax.broadcasted_iota(jnp.int32, (tq, tk), 0)


def _merge_pair(states):
    lane = lax.broadcasted_iota(jnp.int32, (1, LANES), 1)
    (_, l0, a0), (_, l1, a1) = states
    return jnp.where(lane < HEAD_DIM, a0 * (1.0 / l0), a1 * (1.0 / l1))


def _attn_prompt_kernel(q_ref, k_ref, v_ref, c_ref, o_ref, ka_ref, vt_ref, qa_ref, m_ref, acc_ref, *, tq):
    qi = pl.program_id(1)
    n_blk = k_ref.shape[0] // tq
    pairs = N_HEADS // 2
    lanes = lambda p: slice(p * LANES, (p + 1) * LANES)

    @pl.when(qi == 0)
    def _():
        c_parts = _c_parts(c_ref[...])
        for h in range(N_HEADS):
            ka_ref[h] = _aug_k(k_ref[:, lanes(h // 2)], c_parts, h)
        ones_row = (lax.broadcasted_iota(jnp.int32, (V_ROWS - HEAD_DIM, tq), 0) == 0).astype(F32)
        for j in range(n_blk):
            for p in range(pairs):
                v_t = v_ref[j * tq:(j + 1) * tq, lanes(p)].T
                for e in range(2):
                    rows = v_t[e * HEAD_DIM:(e + 1) * HEAD_DIM, :]
                    vt_ref[j, 2 * p + e] = jnp.concatenate([rows, ones_row], axis=0).astype(BF16)

    row0 = pl.multiple_of(qi * tq, tq)
    cq_parts = _c_parts(c_ref[pl.ds(row0, tq), :])
    for h in range(N_HEADS):
        qa_ref[h] = _aug_q(q_ref[:, lanes(h // 2)], cq_parts, h)
    m_ref[...] = jnp.full(m_ref.shape, NEG_BIG, F32)
    acc_ref[...] = jnp.zeros(acc_ref.shape, F32)

    def block(j, mask):
        r = pl.multiple_of(j * tq, tq)

        def logits(h):
            return lax.dot_general(ka_ref[h, pl.ds(r, tq), :], qa_ref[h], (((1,), (1,)), ((), ())),
                                   preferred_element_type=F32)

        s_next = logits(0)
        for h in range(N_HEADS):
            s = s_next
            if h + 1 < N_HEADS:
                s_next = logits(h + 1)
            if mask is not None:
                s = jnp.where(mask, s, NEG_BIG)
            m_old = m_ref[h]
            m_new = jnp.maximum(m_old, jnp.max(s, axis=0, keepdims=True))
            alpha = jnp.exp2(m_old - m_new)
            pt = jnp.exp2(s - m_new).astype(BF16)
            acc_ref[h] = alpha * acc_ref[h] + jnp.dot(vt_ref[j, h], pt, preferred_element_type=F32)
            m_ref[h] = m_new

    def full_block(j, carry):
        block(j, None)
        return carry

    lax.fori_loop(0, qi, full_block, 0)
    key = lax.broadcasted_iota(jnp.int32, (tq, tq), 0)
    query = lax.broadcasted_iota(jnp.int32, (tq, tq), 1)
    block(qi, key <= query)

    def normalized(h):
        return acc_ref[h, 0:HEAD_DIM, :] * (1.0 / acc_ref[h, HEAD_DIM:HEAD_DIM + 1, :])

    for p in range(pairs):
        out_t = jnp.concatenate([normalized(2 * p), normalized(2 * p + 1)], axis=0)
        o_ref[:, lanes(p)] = out_t.T.astype(o_ref.dtype)


def _attn_prompt_call(q, k, v, c):
    b, t, _ = q.shape
    tq = min(512, t)
    per_b = lambda cols: pl.BlockSpec((None, t, cols), lambda bi, qi: (bi, 0, 0))
    tile = pl.BlockSpec((None, tq, D_ATT), lambda bi, qi: (bi, qi, 0))
    return pl.pallas_call(
        functools.partial(_attn_prompt_kernel, tq=tq),
        grid=(b, t // tq),
        in_specs=[tile, per_b(D_ATT), per_b(D_ATT), per_b(N_HEADS)],
        out_specs=tile,
        out_shape=jax.ShapeDtypeStruct((b, t, D_ATT), BF16),
        scratch_shapes=[pltpu.VMEM((N_HEADS, t, LANES), BF16),
                        pltpu.VMEM((t // tq, N_HEADS, V_ROWS, tq), BF16),
                        pltpu.VMEM((N_HEADS, tq, LANES), BF16),
                        pltpu.VMEM((N_HEADS, 1, tq), F32),
                        pltpu.VMEM((N_HEADS, V_ROWS, tq), F32)],
        compiler_params=_params("parallel", "arbitrary"),
    )(q, k, v, c)


def _attn_sample_kernel(q_ref, kc_ref, vc_ref, cc_ref, kn_ref, vn_ref, cn_ref, o_ref, *, tk):
    tq = q_ref.shape[0]
    n_past = kc_ref.shape[0]
    mask = _causal_mask(tq, tq)
    cn_parts = _c_parts(cn_ref[...])
    for p in range(N_HEADS // 2):
        lanes = slice(p * LANES, (p + 1) * LANES)
        states = []
        for e in range(2):
            h = 2 * p + e
            qa = _aug_q(q_ref[:, lanes], cn_parts, h)

            def step(j, state, h=h, qa=qa, lanes=lanes):
                r = pl.multiple_of(j * tk, tk)
                ka = _aug_k(kc_ref[pl.ds(r, tk), lanes], _c_parts(cc_ref[pl.ds(r, tk), :]), h)
                return _attend(qa, ka, vc_ref[pl.ds(r, tk), lanes].astype(BF16), state)

            state = lax.fori_loop(0, n_past // tk, step, _init_state(tq))
            ka = _aug_k(kn_ref[:, lanes], cn_parts, h)
            states.append(_attend(qa, ka, vn_ref[:, lanes].astype(BF16), state, mask))
        o_ref[:, lanes] = _merge_pair(states).astype(o_ref.dtype)


def _attn_sample_call(q, k_cache, v_cache, c_cache, k_new, v_new, c_new):
    b, tq, _ = q.shape
    n_past = k_cache.shape[1]
    tk = min(1024, n_past)
    blk = lambda r, cols: pl.BlockSpec((None, r, cols), lambda bi: (bi, 0, 0))
    return pl.pallas_call(
        functools.partial(_attn_sample_kernel, tk=tk),
        grid=(b,),
        in_specs=[blk(tq, D_ATT), blk(n_past, D_ATT), blk(n_past, D_ATT), blk(n_past, N_HEADS),
                  blk(tq, D_ATT), blk(tq, D_ATT), blk(tq, N_HEADS)],
        out_specs=blk(tq, D_ATT),
        out_shape=jax.ShapeDtypeStruct((b, tq, D_ATT), BF16),
        compiler_params=_params("parallel"),
    )(q, k_cache, v_cache, c_cache, k_new, v_new, c_new)


def _cumsum_kernel(x_ref, tri_ref, c_ref, carry_ref):
    @pl.when(pl.program_id(1) == 0)
    def _():
        carry_ref[...] = jnp.zeros_like(carry_ref)

    tt = x_ref.shape[0]
    c = carry_ref[...]
    for part in _split3(x_ref[...]):
        c = c + jnp.dot(tri_ref[...], part, preferred_element_type=F32)
    c_ref[...] = c[:, 0:N_HEADS]
    carry_ref[...] = c[tt - 1:tt, :]


def _cumsum_call(x):
    b, s, _ = x.shape
    tt = min(512, s)
    tri = jnp.tril(jnp.ones((tt, tt), BF16))
    return pl.pallas_call(
        _cumsum_kernel,
        grid=(b, s // tt),
        in_specs=[pl.BlockSpec((None, tt, LANES), lambda bi, ti: (bi, ti, 0)), _resident(tri.shape)],
        out_specs=pl.BlockSpec((None, tt, N_HEADS), lambda bi, ti: (bi, ti, 0)),
        out_shape=jax.ShapeDtypeStruct((b, s, N_HEADS), F32),
        scratch_shapes=[pltpu.VMEM((1, LANES), F32)],
        compiler_params=_params("parallel", "arbitrary"),
    )(x, tri)


def _pack_ffn(norm, w_gu, w_down):
    return (norm.reshape(1, -1), w_gu.astype(BF16), w_down.astype(BF16))


def _pack_mixer(mix_norm, w_in, w_out, pool_w, pool_scale, conv_w, conv_b, ln_g, ln_b, q_norm, k_norm, forget_b):
    d = w_in.shape[0]
    w_in_p = jnp.pad(w_in, ((0, 0), (0, D_IN_PAD - w_in.shape[1]))).astype(BF16)
    wbd = jax.scipy.linalg.block_diag(*[pool_w[g] for g in range(pool_w.shape[0])]).astype(BF16)
    group = lax.broadcasted_iota(jnp.int32, (256, 256), 0) // HEAD_DIM
    gsum = (group == group.T).astype(BF16)
    row = lambda a: a.reshape(1, -1)
    return dict(
        norm=row(mix_norm), w_in=w_in_p, w_out=w_out.astype(BF16), wbd=wbd, pscale=row(pool_scale),
        conv_w=conv_w, conv_b=row(conv_b), ln_g=row(ln_g), ln_b=row(ln_b),
        q_gain=row(jnp.tile(q_norm, N_HEADS)), k_gain=row(jnp.tile(k_norm, N_HEADS)),
        fb=row(jnp.pad(forget_b, (0, LANES - N_HEADS))), gsum=gsum)


def _pad_rows(a, rows):
    return jnp.pad(a, ((0, 0), (rows - a.shape[1], 0), (0, 0)))


def _layer(x, ffn1, mw, ffn2, pool_prev, conv_prev, cache, t0):
    b, t, d = x.shape
    x1 = _ffn_call(x.reshape(b * t, d), ffn1)
    if cache is None:
        c0 = jnp.zeros((b, 1, LANES), F32)
    else:
        k_past, v_past, logf_past = cache
        n_past = k_past.shape[1]
        c_past = _cumsum_call(jnp.pad(logf_past, ((0, 0), (0, 0), (0, LANES - N_HEADS))))
        c0 = jnp.pad(c_past[:, n_past - 1:, :], ((0, 0), (0, 0), (0, LANES - N_HEADS)))
    ypc, q, k, v, logf, c, ptail, ctail = _mixer_in_call(
        x1.reshape(b, t, d), mw, _pad_rows(pool_prev, POOL_HIST), _pad_rows(conv_prev, CONV_HIST), c0, t0)
    if cache is None:
        yat = _attn_prompt_call(q, k, v, c)
    else:
        yat = _attn_sample_call(q, k_past.reshape(b, n_past, D_ATT), v_past.reshape(b, n_past, D_ATT),
                                c_past, k, v, c)
    y = _ffn_call(x1, ffn2, mix=(ypc.reshape(b * t, -1), yat.reshape(b * t, -1), mw["w_out"]))
    heads = lambda a: a.reshape(b, t, N_HEADS, HEAD_DIM)
    return (y.reshape(b, t, d), ptail[:, POOL_HIST - POOL_PAD:], ctail[:, CONV_HIST - CONV_PAD:],
            heads(k), heads(v), logf)


def kernel(x_prompt, x_sample, state_pool, state_conv, cache_k, cache_v, cache_logf, ffn1_norm, ffn1_w_gu, ffn1_w_down, mix_norm, w_in, w_out, pool_w, pool_scale, conv_w, conv_b, conv_ln_g, conv_ln_b, q_norm, k_norm, forget_b, ffn2_norm, ffn2_w_gu, ffn2_w_down):
    xp, xs = x_prompt, x_sample
    bp = xp.shape[0]
    depth = w_in.shape[0]
    n_past = cache_k.shape[2]
    outs_p, outs_s = [], []
    for l in range(depth):
        ffn1 = _pack_ffn(ffn1_norm[l], ffn1_w_gu[l], ffn1_w_down[l])
        ffn2 = _pack_ffn(ffn2_norm[l], ffn2_w_gu[l], ffn2_w_down[l])
        mw = _pack_mixer(mix_norm[l], w_in[l], w_out[l], pool_w[l], pool_scale[l], conv_w[l], conv_b[l],
                         conv_ln_g[l], conv_ln_b[l], q_norm[l], k_norm[l], forget_b[l])
        xp, *rest = _layer(xp, ffn1, mw, ffn2, jnp.zeros((bp, POOL_PAD, D_POOL), F32),
                           jnp.zeros((bp, CONV_PAD, D_CONV), F32), None, 0)
        outs_p.append(rest)
        xs, *rest = _layer(xs, ffn1, mw, ffn2, state_pool[l], state_conv[l],
                           (cache_k[l], cache_v[l], cache_logf[l]), n_past)
        outs_s.append(rest)
    stack = lambda outs, i: jnp.stack([o[i] for o in outs])
    return (xp, xs,
            stack(outs_p, 0), stack(outs_s, 0),
            stack(outs_p, 1), stack(outs_s, 1),
            stack(outs_p, 2), stack(outs_p, 3), stack(outs_p, 4),
            stack(outs_s, 2), stack(outs_s, 3), stack(outs_s, 4))
```

```python
import functools

import jax
import jax.numpy as jnp
from jax import lax
from jax.experimental import pallas as pl
from jax.experimental.pallas import tpu as pltpu

F32 = jnp.float32
BF16 = jnp.bfloat16

EPS = 1e-6
N_HEADS = 8
HEAD_DIM = 64
V_ROWS = HEAD_DIM + 16
D_ATT = N_HEADS * HEAD_DIM
D_POOL = 256
D_CONV = 256
POOL_WINDOWS = (2, 4, 8, 16)
POOL_GC = D_POOL // len(POOL_WINDOWS)
POOL_PAD = 15
CONV_WIDTH = 31
CONV_PAD = 30
POOL_HIST = 16
CONV_HIST = 32
LANES = 128
SUBLANES = 8
FF_CHUNK = 256
VMEM_LIMIT_BYTES = 56 * 1024 * 1024
NEG_BIG = -1e30
LOG2E = 1.4426950408889634

C_POOL = 0
C_GLU_A = C_POOL + D_POOL
C_GLU_G = C_GLU_A + D_CONV
C_Q = C_GLU_G + D_CONV
C_K = C_Q + D_ATT
C_V = C_K + D_ATT
C_F = C_V + D_ATT
D_IN_PAD = C_F + LANES


def _rms(x, g):
    return x * lax.rsqrt(jnp.mean(x * x, axis=-1, keepdims=True) + EPS) * g


def _split3(x):
    hi = x.astype(BF16)
    r1 = x - hi.astype(F32)
    mid = r1.astype(BF16)
    lo = (r1 - mid.astype(F32)).astype(BF16)
    return hi, mid, lo


def _resident(shape):
    zeros = (0,) * len(shape)
    return pl.BlockSpec(shape, lambda *_: zeros, pipeline_mode=pl.Buffered(1))


def _params(*sem):
    return pltpu.CompilerParams(dimension_semantics=sem, vmem_limit_bytes=VMEM_LIMIT_BYTES)


def _swiglu(x, g_ref, wgu_ref, wd_ref):
    xn = _rms(x, g_ref[...]).astype(BF16)
    f = wd_ref.shape[0]
    acc = None
    for lo in range(0, f, FF_CHUNK):
        a = jnp.dot(xn, wgu_ref[:, lo:lo + FF_CHUNK], preferred_element_type=F32)
        u = jnp.dot(xn, wgu_ref[:, f + lo:f + lo + FF_CHUNK], preferred_element_type=F32)
        h = (a * jax.nn.sigmoid(a) * u).astype(BF16)
        d = jnp.dot(h, wd_ref[lo:lo + FF_CHUNK, :], preferred_element_type=F32)
        acc = d if acc is None else acc + d
    return x + 0.5 * acc


def _ffn_kernel(x_ref, g_ref, wgu_ref, wd_ref, o_ref):
    o_ref[...] = _swiglu(x_ref[...], g_ref, wgu_ref, wd_ref)


def _mix_ffn_kernel(x_ref, ypc_ref, yat_ref, wo_ref, g_ref, wgu_ref, wd_ref, o_ref):
    d_pc = ypc_ref.shape[-1]
    x = x_ref[...]
    x = x + jnp.dot(ypc_ref[...], wo_ref[0:d_pc, :], preferred_element_type=F32)
    x = x + jnp.dot(yat_ref[...], wo_ref[d_pc:, :], preferred_element_type=F32)
    o_ref[...] = _swiglu(x, g_ref, wgu_ref, wd_ref)


def _ffn_call(x, ffn_w, mix=None):
    m, d = x.shape
    tm = min(512, m)
    row = lambda c: pl.BlockSpec((tm, c), lambda i: (i, 0))
    w_specs = [_resident(w.shape) for w in ffn_w]
    if mix is None:
        body, args, specs = _ffn_kernel, (x,), [row(d)]
    else:
        ypc, yat, wo = mix
        body, args = _mix_ffn_kernel, (x, ypc, yat, wo)
        specs = [row(d), row(ypc.shape[1]), row(yat.shape[1]), _resident(wo.shape)]
    return pl.pallas_call(
        body,
        grid=(m // tm,),
        in_specs=specs + w_specs,
        out_specs=row(d),
        out_shape=jax.ShapeDtypeStruct((m, d), F32),
        compiler_params=_params("parallel"),
    )(*args, *ffn_w)


def _head_rms(u, gsum, gain):
    half = gsum.shape[0]
    outs = []
    for s in range(u.shape[1] // half):
        xh = u[:, s * half:(s + 1) * half]
        ss = jnp.dot((xh * xh).astype(BF16), gsum, preferred_element_type=F32)
        outs.append(xh * lax.rsqrt(ss * (1.0 / HEAD_DIM) + EPS))
    return jnp.concatenate(outs, axis=-1) * gain


def _mixer_in_kernel(x_ref, g_ref, win_ref, pprev_ref, cprev_ref, c0_ref, wbd_ref, pscale_ref,
                     convw_ref, convb_ref, lng_ref, lnb_ref, qg_ref, kg_ref, fb_ref, gsum_ref, tri_ref,
                     ypc_ref, q_ref, k_ref, v_ref, logf_ref, c_ref, ptail_ref, ctail_ref,
                     psum_ref, cseq_ref, cshift_ref, carry_ref, *, tt, t0):
    ti = pl.program_id(1)
    cur = SUBLANES + POOL_HIST

    @pl.when(ti == 0)
    def _():
        for i in range(len(POOL_WINDOWS)):
            psum_ref[i, 0:SUBLANES, :] = jnp.zeros((SUBLANES, D_POOL), F32)
        psum_ref[0, SUBLANES:cur, :] = pprev_ref[...]
        cseq_ref[0:CONV_HIST, :] = cprev_ref[...]
        carry_ref[...] = c0_ref[...]

    xn = _rms(x_ref[...], g_ref[...]).astype(BF16)

    def proj(lo, hi):
        return jnp.dot(xn, win_ref[:, lo:hi], preferred_element_type=F32)

    up = proj(C_POOL, C_GLU_A)
    psum_ref[0, cur:cur + tt, :] = up
    n = POOL_HIST + tt
    win_sum = {}
    for i, w in enumerate(POOL_WINDOWS):
        half = w // 2
        s = psum_ref[i, SUBLANES:SUBLANES + n, :] + psum_ref[i, pl.ds(SUBLANES - half, n), :]
        win_sum[w] = s[POOL_HIST:, :]
        if i + 1 < len(POOL_WINDOWS):
            psum_ref[i + 1, SUBLANES:SUBLANES + n, :] = s
    pos = t0 + ti * tt + lax.broadcasted_iota(jnp.int32, (tt, 1), 0)
    lane = lax.broadcasted_iota(jnp.int32, (1, D_POOL), 1)
    pooled = None
    for gi in reversed(range(len(POOL_WINDOWS))):
        w = POOL_WINDOWS[gi]
        mean = win_sum[w] * (1.0 / jnp.minimum(pos + 1, w).astype(F32))
        pooled = mean if pooled is None else jnp.where(lane < (gi + 1) * POOL_GC, mean, pooled)
    d = (pooled - up).astype(BF16)
    y_pool = jnp.dot(d, wbd_ref[...], preferred_element_type=F32) * pscale_ref[...]
    ypc_ref[:, 0:D_POOL] = y_pool.astype(ypc_ref.dtype)
    ptail_ref[...] = up[tt - POOL_HIST:tt, :]
    psum_ref[0, SUBLANES:cur, :] = psum_ref[0, tt + SUBLANES:tt + cur, :]

    z = proj(C_GLU_A, C_GLU_G) * jax.nn.sigmoid(proj(C_GLU_G, C_Q))
    cseq_ref[CONV_HIST:CONV_HIST + tt, :] = z
    n_sh = tt + CONV_HIST - SUBLANES
    for b in range(1, SUBLANES):
        cshift_ref[b - 1] = cseq_ref[pl.ds(b, n_sh), :]
    yc = jnp.broadcast_to(convb_ref[...], (tt, D_CONV))
    for j in range(CONV_WIDTH):
        a, b = divmod(CONV_HIST - CONV_PAD + j, SUBLANES)
        rows = slice(a * SUBLANES, a * SUBLANES + tt)
        tap = cseq_ref[rows, :] if b == 0 else cshift_ref[b - 1, rows, :]
        yc = yc + convw_ref[j:j + 1, :] * tap
    mu = jnp.mean(yc, axis=-1, keepdims=True)
    yc = yc - mu
    var = jnp.mean(yc * yc, axis=-1, keepdims=True)
    yc = yc * lax.rsqrt(var + EPS) * lng_ref[...] + lnb_ref[...]
    ypc_ref[:, D_POOL:D_POOL + D_CONV] = (yc * jax.nn.sigmoid(yc)).astype(ypc_ref.dtype)
    ctail_ref[...] = z[tt - CONV_HIST:tt, :]
    cseq_ref[0:CONV_HIST, :] = cseq_ref[tt:tt + CONV_HIST, :]

    gsum = gsum_ref[...]
    q_ref[...] = (_head_rms(proj(C_Q, C_K), gsum, qg_ref[...]) * (HEAD_DIM ** -0.5 * LOG2E)).astype(q_ref.dtype)
    k_ref[...] = _head_rms(proj(C_K, C_V), gsum, kg_ref[...])
    v_ref[...] = proj(C_V, C_F)
    xf = proj(C_F, D_IN_PAD) + fb_ref[...]
    lf = jnp.minimum(xf, 0.0) - jnp.log1p(jnp.exp(-jnp.abs(xf)))
    logf_ref[...] = lf[:, 0:N_HEADS]
    tri = tri_ref[...]
    c = carry_ref[...]
    for part in _split3(lf):
        c = c + jnp.dot(tri, part, preferred_element_type=F32)
    c_ref[...] = c[:, 0:N_HEADS]
    carry_ref[...] = c[tt - 1:tt, :]


def _mixer_in_call(x, mw, pool_prev, conv_prev, c0, t0):
    b, t, d = x.shape
    tt = min(512, t)
    tri = jnp.tril(jnp.ones((tt, tt), BF16))
    weights = (mw["norm"], mw["w_in"])
    small = (mw["wbd"], mw["pscale"], mw["conv_w"], mw["conv_b"], mw["ln_g"], mw["ln_b"],
             mw["q_gain"], mw["k_gain"], mw["fb"], mw["gsum"], tri)
    per_b = lambda r, c: pl.BlockSpec((None, r, c), lambda bi, ti: (bi, 0, 0))
    tile = lambda c: pl.BlockSpec((None, tt, c), lambda bi, ti: (bi, ti, 0))
    out_shape = (
        jax.ShapeDtypeStruct((b, t, D_POOL + D_CONV), BF16),
        jax.ShapeDtypeStruct((b, t, D_ATT), BF16),
        jax.ShapeDtypeStruct((b, t, D_ATT), F32),
        jax.ShapeDtypeStruct((b, t, D_ATT), F32),
        jax.ShapeDtypeStruct((b, t, N_HEADS), F32),
        jax.ShapeDtypeStruct((b, t, N_HEADS), F32),
        jax.ShapeDtypeStruct((b, POOL_HIST, D_POOL), F32),
        jax.ShapeDtypeStruct((b, CONV_HIST, D_CONV), F32),
    )
    out_specs = (tile(D_POOL + D_CONV), tile(D_ATT), tile(D_ATT), tile(D_ATT), tile(N_HEADS), tile(N_HEADS),
                 per_b(POOL_HIST, D_POOL), per_b(CONV_HIST, D_CONV))
    return pl.pallas_call(
        functools.partial(_mixer_in_kernel, tt=tt, t0=t0),
        grid=(b, t // tt),
        in_specs=[tile(d)] + [_resident(w.shape) for w in weights]
        + [per_b(POOL_HIST, D_POOL), per_b(CONV_HIST, D_CONV), per_b(1, LANES)]
        + [_resident(w.shape) for w in small],
        out_specs=out_specs,
        out_shape=out_shape,
        scratch_shapes=[pltpu.VMEM((len(POOL_WINDOWS), SUBLANES + POOL_HIST + tt, D_POOL), F32),
                        pltpu.VMEM((CONV_HIST + tt, D_CONV), F32),
                        pltpu.VMEM((SUBLANES - 1, CONV_HIST - SUBLANES + tt, D_CONV), F32),
                        pltpu.VMEM((1, LANES), F32)],
        compiler_params=_params("parallel", "arbitrary"),
    )(x, *weights, pool_prev, conv_prev, c0, *small)


def _aug_lanes(e):
    base = HEAD_DIM * (1 - e)
    return base, lax.broadcasted_iota(jnp.int32, (1, LANES), 1)


def _c_parts(c):
    return [part.astype(F32) for part in _split3(c * LOG2E)]


def _aug_k(k_pair, c_parts, h):
    e = h % 2
    base, lane = _aug_lanes(e)
    in_head = (lane >= HEAD_DIM * e) & (lane < HEAD_DIM * (e + 1))
    out = jnp.where(in_head, k_pair, 0.0)
    for i, part in enumerate(c_parts):
        out = jnp.where(lane == base + i, -part[:, h:h + 1], out)
    out = jnp.where((lane >= base + 3) & (lane < base + 6), 1.0, out)
    return out.astype(BF16)


def _aug_q(q_pair, c_parts, h):
    e = h % 2
    base, lane = _aug_lanes(e)
    in_head = (lane >= HEAD_DIM * e) & (lane < HEAD_DIM * (e + 1))
    out = jnp.where(in_head, q_pair.astype(F32), 0.0)
    out = jnp.where((lane >= base) & (lane < base + 3), 1.0, out)
    for i, part in enumerate(c_parts):
        out = jnp.where(lane == base + 3 + i, part[:, h:h + 1], out)
    return out.astype(BF16)


def _attend(qa, ka, vb, state, mask=None):
    m, l, acc = state
    s = lax.dot_general(qa, ka, (((1,), (1,)), ((), ())), preferred_element_type=F32)
    if mask is not None:
        s = jnp.where(mask, s, NEG_BIG)
    m_new = jnp.maximum(m, jnp.max(s, axis=-1, keepdims=True))
    alpha = jnp.exp2(m - m_new)
    p = jnp.exp2(s - m_new)
    l = alpha * l + jnp.sum(p, axis=-1, keepdims=True)
    acc = alpha * acc + jnp.dot(p.astype(BF16), vb, preferred_element_type=F32)
    return m_new, l, acc


def _init_state(tq):
    return (jnp.full((tq, 1), NEG_BIG, F32), jnp.zeros((tq, 1), F32), jnp.zeros((tq, LANES), F32))


def _causal_mask(tq, tk):
    return lax.broadcasted_iota(jnp.int32, (tq, tk), 1) <= lax.broadcasted_iota(jnp.int32, (tq, tk), 0)


def _merge_pair(states):
    lane = lax.broadcasted_iota(jnp.int32, (1, LANES), 1)
    (_, l0, a0), (_, l1, a1) = states
    return jnp.where(lane < HEAD_DIM, a0 * (1.0 / l0), a1 * (1.0 / l1))


def _attn_prompt_kernel(q_ref, k_ref, v_ref, c_ref, o_ref, ka_ref, vt_ref, qa_ref, m_ref, acc_ref, *, tq):
    qi = pl.program_id(1)
    n_blk = k_ref.shape[0] // tq
    pairs = N_HEADS // 2
    lanes = lambda p: slice(p * LANES, (p + 1) * LANES)

    @pl.when(qi == 0)
    def _():
        c_parts = _c_parts(c_ref[...])
        for h in range(N_HEADS):
            ka_ref[h] = _aug_k(k_ref[:, lanes(h // 2)], c_parts, h)
        ones_row = (lax.broadcasted_iota(jnp.int32, (V_ROWS - HEAD_DIM, tq), 0) == 0).astype(F32)
        for j in range(n_blk):
            for p in range(pairs):
                v_t = v_ref[j * tq:(j + 1) * tq, lanes(p)].T
                for e in range(2):
                    rows = v_t[e * HEAD_DIM:(e + 1) * HEAD_DIM, :]
                    vt_ref[j, 2 * p + e] = jnp.concatenate([rows, ones_row], axis=0).astype(BF16)

    row0 = pl.multiple_of(qi * tq, tq)
    cq_parts = _c_parts(c_ref[pl.ds(row0, tq), :])
    for h in range(N_HEADS):
        qa_ref[h] = _aug_q(q_ref[:, lanes(h // 2)], cq_parts, h)
    m_ref[...] = jnp.full(m_ref.shape, NEG_BIG, F32)
    acc_ref[...] = jnp.zeros(acc_ref.shape, F32)

    def block(j, mask):
        r = pl.multiple_of(j * tq, tq)

        def logits(h):
            return lax.dot_general(ka_ref[h, pl.ds(r, tq), :], qa_ref[h], (((1,), (1,)), ((), ())),
                                   preferred_element_type=F32)

        ahead = 2
        pending = [logits(h) for h in range(ahead)]
        for h in range(N_HEADS):
            s = pending.pop(0)
            if h + ahead < N_HEADS:
                pending.append(logits(h + ahead))
            if mask is not None:
                s = jnp.where(mask, s, NEG_BIG)
            m_old = m_ref[h]
            m_new = jnp.maximum(m_old, jnp.max(s, axis=0, keepdims=True))
            alpha = jnp.exp2(m_old - m_new)
            pt = jnp.exp2(s - m_new).astype(BF16)
            acc_ref[h] = alpha * acc_ref[h] + jnp.dot(vt_ref[j, h], pt, preferred_element_type=F32)
            m_ref[h] = m_new

    def full_block(j, carry):
        block(j, None)
        return carry

    lax.fori_loop(0, qi, full_block, 0)
    key = lax.broadcasted_iota(jnp.int32, (tq, tq), 0)
    query = lax.broadcasted_iota(jnp.int32, (tq, tq), 1)
    block(qi, key <= query)

    def normalized(h):
        return acc_ref[h, 0:HEAD_DIM, :] * (1.0 / acc_ref[h, HEAD_DIM:HEAD_DIM + 1, :])

    for p in range(pairs):
        out_t = jnp.concatenate([normalized(2 * p), normalized(2 * p + 1)], axis=0)
        o_ref[:, lanes(p)] = out_t.T.astype(o_ref.dtype)


def _attn_prompt_call(q, k, v, c):
    b, t, _ = q.shape
    tq = min(512, t)
    per_b = lambda cols: pl.BlockSpec((None, t, cols), lambda bi, qi: (bi, 0, 0))
    tile = pl.BlockSpec((None, tq, D_ATT), lambda bi, qi: (bi, qi, 0))
    return pl.pallas_call(
        functools.partial(_attn_prompt_kernel, tq=tq),
        grid=(b, t // tq),
        in_specs=[tile, per_b(D_ATT), per_b(D_ATT), per_b(N_HEADS)],
        out_specs=tile,
        out_shape=jax.ShapeDtypeStruct((b, t, D_ATT), BF16),
        scratch_shapes=[pltpu.VMEM((N_HEADS, t, LANES), BF16),
                        pltpu.VMEM((t // tq, N_HEADS, V_ROWS, tq), BF16),
                        pltpu.VMEM((N_HEADS, tq, LANES), BF16),
                        pltpu.VMEM((N_HEADS, 1, tq), F32),
                        pltpu.VMEM((N_HEADS, V_ROWS, tq), F32)],
        compiler_params=_params("parallel", "arbitrary"),
    )(q, k, v, c)


def _attn_sample_kernel(q_ref, kc_ref, vc_ref, cc_ref, kn_ref, vn_ref, cn_ref, o_ref, parts_ref, *, tk):
    tq = q_ref.shape[0]
    n_past = kc_ref.shape[0]
    mask = _causal_mask(tq, tq)
    cn_parts = _c_parts(cn_ref[...])
    for i, part in enumerate(_c_parts(cc_ref[...])):
        parts_ref[i] = part
    for p in range(N_HEADS // 2):
        lanes = slice(p * LANES, (p + 1) * LANES)
        states = []
        for e in range(2):
            h = 2 * p + e
            qa = _aug_q(q_ref[:, lanes], cn_parts, h)

            def step(j, state, h=h, qa=qa, lanes=lanes):
                r = pl.multiple_of(j * tk, tk)
                c_parts = [parts_ref[i, pl.ds(r, tk), :] for i in range(3)]
                ka = _aug_k(kc_ref[pl.ds(r, tk), lanes], c_parts, h)
                return _attend(qa, ka, vc_ref[pl.ds(r, tk), lanes].astype(BF16), state)

            state = lax.fori_loop(0, n_past // tk, step, _init_state(tq))
            ka = _aug_k(kn_ref[:, lanes], cn_parts, h)
            states.append(_attend(qa, ka, vn_ref[:, lanes].astype(BF16), state, mask))
        o_ref[:, lanes] = _merge_pair(states).astype(o_ref.dtype)


def _attn_sample_call(q, k_cache, v_cache, c_cache, k_new, v_new, c_new):
    b, tq, _ = q.shape
    n_past = k_cache.shape[1]
    tk = min(1024, n_past)
    blk = lambda r, cols: pl.BlockSpec((None, r, cols), lambda bi: (bi, 0, 0))
    return pl.pallas_call(
        functools.partial(_attn_sample_kernel, tk=tk),
        grid=(b,),
        in_specs=[blk(tq, D_ATT), blk(n_past, D_ATT), blk(n_past, D_ATT), blk(n_past, N_HEADS),
                  blk(tq, D_ATT), blk(tq, D_ATT), blk(tq, N_HEADS)],
        out_specs=blk(tq, D_ATT),
        out_shape=jax.ShapeDtypeStruct((b, tq, D_ATT), BF16),
        scratch_shapes=[pltpu.VMEM((3, n_past, N_HEADS), F32)],
        compiler_params=_params("parallel"),
    )(q, k_cache, v_cache, c_cache, k_new, v_new, c_new)


def _cumsum_kernel(x_ref, tri_ref, c_ref, carry_ref):
    @pl.when(pl.program_id(1) == 0)
    def _():
        carry_ref[...] = jnp.zeros_like(carry_ref)

    tt = x_ref.shape[0]
    c = carry_ref[...]
    for part in _split3(x_ref[...]):
        c = c + jnp.dot(tri_ref[...], part, preferred_element_type=F32)
    c_ref[...] = c[:, 0:N_HEADS]
    carry_ref[...] = c[tt - 1:tt, :]


def _cumsum_call(x):
    b, s, _ = x.shape
    tt = min(512, s)
    tri = jnp.tril(jnp.ones((tt, tt), BF16))
    return pl.pallas_call(
        _cumsum_kernel,
        grid=(b, s // tt),
        in_specs=[pl.BlockSpec((None, tt, LANES), lambda bi, ti: (bi, ti, 0)), _resident(tri.shape)],
        out_specs=pl.BlockSpec((None, tt, N_HEADS), lambda bi, ti: (bi, ti, 0)),
        out_shape=jax.ShapeDtypeStruct((b, s, N_HEADS), F32),
        scratch_shapes=[pltpu.VMEM((1, LANES), F32)],
        compiler_params=_params("parallel", "arbitrary"),
    )(x, tri)


def _pack_ffn(norm, w_gu, w_down):
    return (norm.reshape(1, -1), w_gu.astype(BF16), w_down.astype(BF16))


def _pack_mixer(mix_norm, w_in, w_out, pool_w, pool_scale, conv_w, conv_b, ln_g, ln_b, q_norm, k_norm, forget_b):
    d = w_in.shape[0]
    w_in_p = jnp.pad(w_in, ((0, 0), (0, D_IN_PAD - w_in.shape[1]))).astype(BF16)
    wbd = jax.scipy.linalg.block_diag(*[pool_w[g] for g in range(pool_w.shape[0])]).astype(BF16)
    group = lax.broadcasted_iota(jnp.int32, (256, 256), 0) // HEAD_DIM
    gsum = (group == group.T).astype(BF16)
    row = lambda a: a.reshape(1, -1)
    return dict(
        norm=row(mix_norm), w_in=w_in_p, w_out=w_out.astype(BF16), wbd=wbd, pscale=row(pool_scale),
        conv_w=conv_w, conv_b=row(conv_b), ln_g=row(ln_g), ln_b=row(ln_b),
        q_gain=row(jnp.tile(q_norm, N_HEADS)), k_gain=row(jnp.tile(k_norm, N_HEADS)),
        fb=row(jnp.pad(forget_b, (0, LANES - N_HEADS))), gsum=gsum)


def _pad_rows(a, rows):
    return jnp.pad(a, ((0, 0), (rows - a.shape[1], 0), (0, 0)))


def _layer(x, ffn1, mw, ffn2, pool_prev, conv_prev, cache, t0):
    b, t, d = x.shape
    x1 = _ffn_call(x.reshape(b * t, d), ffn1)
    if cache is None:
        c0 = jnp.zeros((b, 1, LANES), F32)
    else:
        k_past, v_past, logf_past = cache
        n_past = k_past.shape[1]
        c_past = _cumsum_call(jnp.pad(logf_past, ((0, 0), (0, 0), (0, LANES - N_HEADS))))
        c0 = jnp.pad(c_past[:, n_past - 1:, :], ((0, 0), (0, 0), (0, LANES - N_HEADS)))
    ypc, q, k, v, logf, c, ptail, ctail = _mixer_in_call(
        x1.reshape(b, t, d), mw, _pad_rows(pool_prev, POOL_HIST), _pad_rows(conv_prev, CONV_HIST), c0, t0)
    if cache is None:
        yat = _attn_prompt_call(q, k, v, c)
    else:
        yat = _attn_sample_call(q, k_past.reshape(b, n_past, D_ATT), v_past.reshape(b, n_past, D_ATT),
                                c_past, k, v, c)
    y = _ffn_call(x1, ffn2, mix=(ypc.reshape(b * t, -1), yat.reshape(b * t, -1), mw["w_out"]))
    heads = lambda a: a.reshape(b, t, N_HEADS, HEAD_DIM)
    return (y.reshape(b, t, d), ptail[:, POOL_HIST - POOL_PAD:], ctail[:, CONV_HIST - CONV_PAD:],
            heads(k), heads(v), logf)


def kernel(x_prompt, x_sample, state_pool, state_conv, cache_k, cache_v, cache_logf, ffn1_norm, ffn1_w_gu, ffn1_w_down, mix_norm, w_in, w_out, pool_w, pool_scale, conv_w, conv_b, conv_ln_g, conv_ln_b, q_norm, k_norm, forget_b, ffn2_norm, ffn2_w_gu, ffn2_w_down):
    xp, xs = x_prompt, x_sample
    bp = xp.shape[0]
    depth = w_in.shape[0]
    n_past = cache_k.shape[2]
    outs_p, outs_s = [], []
    for l in range(depth):
        ffn1 = _pack_ffn(ffn1_norm[l], ffn1_w_gu[l], ffn1_w_down[l])
        ffn2 = _pack_ffn(ffn2_norm[l], ffn2_w_gu[l], ffn2_w_down[l])
        mw = _pack_mixer(mix_norm[l], w_in[l], w_out[l], pool_w[l], pool_scale[l], conv_w[l], conv_b[l],
                         conv_ln_g[l], conv_ln_b[l], q_norm[l], k_norm[l], forget_b[l])
        xp, *rest = _layer(xp, ffn1, mw, ffn2, jnp.zeros((bp, POOL_PAD, D_POOL), F32),
                           jnp.zeros((bp, CONV_PAD, D_CONV), F32), None, 0)
        outs_p.append(rest)
        xs, *rest = _layer(xs, ffn1, mw, ffn2, state_pool[l], state_conv[l],
                           (cache_k[l], cache_v[l], cache_logf[l]), n_past)
        outs_s.append(rest)
    stack = lambda outs, i: jnp.stack([o[i] for o in outs])
    return (xp, xs,
            stack(outs_p, 0), stack(outs_s, 0),
            stack(outs_p, 1), stack(outs_s, 1),
            stack(outs_p, 2), stack(outs_p, 3), stack(outs_p, 4),
            stack(outs_s, 2), stack(outs_s, 3), stack(outs_s, 4))
```
